```python
import jax, jax.numpy as jnp
from jax import lax
import numpy as np

D_MODEL = 2048
BATCH = 2
SEQ = 4096
DEPTH = 2

HEAD_DIM = 128
DIL_PATTERNS = ((128, 1), (512, 4), (2048, 16))
A_HEADS_PER_GROUP = 4
A_GROUPS = len(DIL_PATTERNS)
A_HEADS = A_HEADS_PER_GROUP * A_GROUPS
A_WIDTH = A_HEADS * HEAD_DIM
A_OUT = A_HEADS_PER_GROUP * HEAD_DIM
ATT_BLOCK = 128
ROT_DIM = HEAD_DIM // 4
ROPE_THETA = 500000.0
GLA_HEADS = 4
GLA_DK = D_MODEL // 2
GLA_DV = D_MODEL
GLA_HK = GLA_DK // GLA_HEADS
GLA_HV = GLA_DV // GLA_HEADS
GLA_LOWRANK = 16
GLA_TAU = 16.0
GLA_CHUNK = 64
IN_SIZES = (A_WIDTH, A_WIDTH, A_WIDTH, GLA_DK, GLA_DK, GLA_DV, GLA_DV, GLA_LOWRANK, D_MODEL, D_MODEL)
N_IN = sum(IN_SIZES)
N_EXPERTS = 16
N_GROUPS = 4
EXPERTS_PER_GROUP = N_EXPERTS // N_GROUPS
TOP_K = 2
D_FF_EXPERT = 1408
DN_ALPHA = (2 * DEPTH) ** 0.25
DN_BETA = (8 * DEPTH) ** -0.25
LN_EPS = 1e-5
RMS_EPS = 1e-6

kernel_name = "hybrid_dilated_gla_groupmoe_deepnorm"


def layer_norm(x, g, b):
    xf = x.astype(jnp.float32)
    mu = jnp.mean(xf, axis=-1, keepdims=True)
    var = jnp.mean(jnp.square(xf - mu), axis=-1, keepdims=True)
    return ((xf - mu) * lax.rsqrt(var + LN_EPS) * g.astype(jnp.float32) + b.astype(jnp.float32)).astype(x.dtype)


def rope_tables(positions):
    inv = ROPE_THETA ** (-jnp.arange(0, ROT_DIM, 2, dtype=jnp.float32) / ROT_DIM)
    ang = positions.astype(jnp.float32)[..., None] * inv
    return jnp.cos(ang)[:, :, None, :], jnp.sin(ang)[:, :, None, :]


def apply_partial_rope(x, cos, sin):
    xr = x[..., :ROT_DIM].astype(jnp.float32)
    x1, x2 = xr[..., :ROT_DIM // 2], xr[..., ROT_DIM // 2:]
    rot = jnp.concatenate([x1 * cos - x2 * sin, x2 * cos + x1 * sin], axis=-1).astype(x.dtype)
    return jnp.concatenate([rot, x[..., ROT_DIM:]], axis=-1)


def dilated_group_attention(q, k, v, window, dil):
    B_, S_, H_, Dh = q.shape
    n_off = window // dil
    L = S_ // dil
    nb = -(-L // ATT_BLOCK)
    Lp = nb * ATT_BLOCK

    def to_sub(t):
        t = t.reshape(B_, L, dil, H_, Dh).transpose(0, 2, 1, 3, 4).reshape(B_ * dil, L, H_, Dh)
        t = jnp.pad(t, ((0, 0), (0, Lp - L), (0, 0), (0, 0)))
        return t.reshape(B_ * dil, nb, ATT_BLOCK, H_, Dh)

    qb, kb, vb = to_sub(q), to_sub(k), to_sub(v)

    def with_prev(t):
        prev = jnp.concatenate([jnp.zeros_like(t[:, :1]), t[:, :-1]], axis=1)
        return jnp.concatenate([prev, t], axis=2)

    kk, vv = with_prev(kb), with_prev(vb)
    s = jnp.einsum('nbqhd,nbkhd->nbhqk', qb, kk).astype(jnp.float32) * (Dh ** -0.5)
    qi = jnp.arange(nb)[:, None] * ATT_BLOCK + jnp.arange(ATT_BLOCK)[None, :]
    kj = jnp.arange(nb)[:, None] * ATT_BLOCK - ATT_BLOCK + jnp.arange(2 * ATT_BLOCK)[None, :]
    rel = qi[:, :, None] - kj[:, None, :]
    mask = (kj[:, None, :] >= 0) & (rel >= 0) & (rel <= n_off)
    s = jnp.where(mask[None, :, None], s, -jnp.inf)
    m = jnp.max(s, axis=-1, keepdims=True)
    p = jnp.exp(s - m)
    den = jnp.sum(p, axis=-1, keepdims=True)
    o = jnp.einsum('nbhqk,nbkhd->nbqhd', (p / den).astype(v.dtype), vv)
    lse = (m + jnp.log(den))[..., 0]
    o = o.reshape(B_, dil, Lp, H_, Dh)[:, :, :L].transpose(0, 2, 1, 3, 4).reshape(B_, S_, H_, Dh)
    lse = lse.transpose(0, 1, 3, 2).reshape(B_, dil, Lp, H_)[:, :, :L].transpose(0, 2, 1, 3).reshape(B_, S_, H_)
    return o, lse


def gla_chunked(q, k, v, log_a):
    B_, S_, H_, DK = q.shape
    DV = v.shape[-1]
    C = GLA_CHUNK
    n = S_ // C
    qc = q.astype(jnp.float32).reshape(B_, n, C, H_, DK) * (DK ** -0.5)
    kc = k.astype(jnp.float32).reshape(B_, n, C, H_, DK)
    vc = v.astype(jnp.float32).reshape(B_, n, C, H_, DV)
    b = jnp.cumsum(log_a.reshape(B_, n, C, H_, DK), axis=2)
    b_last = b[:, :, -1:]
    q_t = qc * jnp.exp(b)
    k_t = kc * jnp.exp(-b)
    k_dec = kc * jnp.exp(b_last - b)
    causal = jnp.tril(jnp.ones((C, C), dtype=bool))
    att = jnp.where(causal, jnp.einsum('bnchk,bnshk->bnhcs', q_t, k_t), 0.0)
    o_intra = jnp.einsum('bnhcs,bnshv->bnchv', att, vc)
    decay = jnp.exp(b_last[:, :, 0])

    def step(state, inp):
        q_i, kd_i, v_i, dec_i = inp
        o_i = jnp.einsum('bchk,bhkv->bchv', q_i, state)
        state = state * dec_i[..., None] + jnp.einsum('bchk,bchv->bhkv', kd_i, v_i)
        return state, o_i

    init = jnp.zeros((B_, H_, DK, DV), jnp.float32)
    _, o_inter = lax.scan(step, init, (jnp.moveaxis(q_t, 1, 0), jnp.moveaxis(k_dec, 1, 0),
                                       jnp.moveaxis(vc, 1, 0), jnp.moveaxis(decay, 1, 0)))
    o = o_intra + jnp.moveaxis(o_inter, 0, 1)
    return o.reshape(B_, S_, H_, DV)


def hybrid_mixer(h, cos, sin, w_in, w_gk2, b_gk, g_gla, p_a, p_b, w_o):
    B_, S_, _ = h.shape
    proj = h @ w_in
    splits = [int(c) for c in np.cumsum(IN_SIZES)[:-1]]
    qa, ka, va, qb, kb, vb, rb, gk_low, ga, gb = jnp.split(proj, splits, axis=-1)

    qa = apply_partial_rope(qa.reshape(B_, S_, A_HEADS, HEAD_DIM), cos, sin)
    ka = apply_partial_rope(ka.reshape(B_, S_, A_HEADS, HEAD_DIM), cos, sin)
    va = va.reshape(B_, S_, A_HEADS, HEAD_DIM)
    outs, lses = [], []
    for g, (window, dil) in enumerate(DIL_PATTERNS):
        hs = slice(g * A_HEADS_PER_GROUP, (g + 1) * A_HEADS_PER_GROUP)
        o_g, lse_g = dilated_group_attention(qa[:, :, hs], ka[:, :, hs], va[:, :, hs], window, dil)
        outs.append(o_g)
        lses.append(lse_g)
    wts = jax.nn.softmax(jnp.stack(lses, axis=0), axis=0)
    y_a = jnp.sum(wts[..., None] * jnp.stack(outs, axis=0).astype(jnp.float32), axis=0)
    y_a = y_a.astype(h.dtype).reshape(B_, S_, A_OUT)

    gk = (gk_low @ w_gk2 + b_gk).astype(jnp.float32)
    log_a = (jax.nn.log_sigmoid(gk) / GLA_TAU).reshape(B_, S_, GLA_HEADS, GLA_HK)
    o_b = gla_chunked(qb.reshape(B_, S_, GLA_HEADS, GLA_HK), kb.reshape(B_, S_, GLA_HEADS, GLA_HK),
                      vb.reshape(B_, S_, GLA_HEADS, GLA_HV), log_a)
    o_b = o_b * lax.rsqrt(jnp.mean(jnp.square(o_b), axis=-1, keepdims=True) + RMS_EPS) * g_gla.astype(jnp.float32)
    y_b = (o_b.reshape(B_, S_, GLA_DV) * jax.nn.silu(rb.astype(jnp.float32))).astype(h.dtype)

    gate_a = jax.nn.sigmoid(ga.astype(jnp.float32)).astype(h.dtype)
    gate_b = jax.nn.sigmoid(gb.astype(jnp.float32)).astype(h.dtype)
    merged = gate_a * (y_a @ p_a) + gate_b * (y_b @ p_b)
    return merged @ w_o


def group_moe(h, router_w, router_b, w_gate, w_up, w_down):
    shape = h.shape
    t = h.reshape(-1, shape[-1])
    logits = (t @ router_w).astype(jnp.float32) + router_b.astype(jnp.float32)
    probs = jax.nn.softmax(logits, axis=-1)
    grouped = probs.reshape(-1, N_GROUPS, EXPERTS_PER_GROUP)
    group_score = jnp.sum(lax.top_k(grouped, TOP_K)[0], axis=-1)
    g_sel = jnp.argmax(group_score, axis=-1)
    in_group = (jnp.arange(N_EXPERTS) // EXPERTS_PER_GROUP)[None, :] == g_sel[:, None]
    top_v, top_i = lax.top_k(jnp.where(in_group, probs, -1.0), TOP_K)
    top_w = top_v / jnp.sum(top_v, axis=-1, keepdims=True)
    combine = jnp.sum(jax.nn.one_hot(top_i, N_EXPERTS, dtype=jnp.float32) * top_w[..., None], axis=1)
    combine = combine.astype(t.dtype)
    y = jnp.zeros_like(t)
    for e in range(N_EXPERTS):
        a = t @ w_gate[e]
        u = t @ w_up[e]
        y = y + combine[:, e:e + 1] * ((jax.nn.silu(a) * u) @ w_down[e])
    return y.reshape(shape)


def setup_inputs(seed: int = 0) -> dict:
    key = jax.random.key(seed)
    ks = jax.random.split(key, 20)
    f32 = jnp.float32
    x = jax.random.normal(ks[0], (BATCH, SEQ, D_MODEL), f32)
    offset = jax.random.randint(ks[1], (BATCH, 1), 0, 1024, dtype=jnp.int32)
    positions = (offset + jnp.arange(SEQ, dtype=jnp.int32)[None, :]).astype(jnp.int32)
    col_scale = np.concatenate([np.full(s, c, np.float32) for s, c in zip(
        IN_SIZES, (1.0, 1.0, DN_BETA, 1.0, 1.0, DN_BETA, 1.0, 1.0, 1.0, 1.0))])
    w_in = jax.random.normal(ks[2], (DEPTH, D_MODEL, N_IN), f32) * (D_MODEL ** -0.5) * jnp.asarray(col_scale)
    w_gk2 = jax.random.normal(ks[3], (DEPTH, GLA_LOWRANK, GLA_DK), f32) * (GLA_LOWRANK ** -0.5)
    b_gk = jax.random.normal(ks[4], (DEPTH, GLA_DK), f32) * 0.1
    g_gla = 1.0 + 0.02 * jax.random.normal(ks[5], (DEPTH, GLA_HV), f32)
    p_a = jax.random.normal(ks[6], (DEPTH, A_OUT, D_MODEL), f32) * (A_OUT ** -0.5)
    p_b = jax.random.normal(ks[7], (DEPTH, GLA_DV, D_MODEL), f32) * (GLA_DV ** -0.5)
    w_o = jax.random.normal(ks[8], (DEPTH, D_MODEL, D_MODEL), f32) * (D_MODEL ** -0.5) * DN_BETA
    ln1_g = 1.0 + 0.02 * jax.random.normal(ks[9], (DEPTH, D_MODEL), f32)
    ln1_b = 0.02 * jax.random.normal(ks[10], (DEPTH, D_MODEL), f32)
    router_w = jax.random.normal(ks[11], (D_MODEL, N_EXPERTS), f32) * (D_MODEL ** -0.5)
    router_b = 0.01 * jax.random.normal(ks[12], (N_EXPERTS,), f32)
    w_gate = jax.random.normal(ks[13], (DEPTH, N_EXPERTS, D_MODEL, D_FF_EXPERT), f32) * (D_MODEL ** -0.5)
    w_up = jax.random.normal(ks[14], (DEPTH, N_EXPERTS, D_MODEL, D_FF_EXPERT), f32) * (D_MODEL ** -0.5)
    w_down = jax.random.normal(ks[15], (DEPTH, N_EXPERTS, D_FF_EXPERT, D_MODEL), f32) * (D_FF_EXPERT ** -0.5) * DN_BETA
    ln2_g = 1.0 + 0.02 * jax.random.normal(ks[16], (DEPTH, D_MODEL), f32)
    ln2_b = 0.02 * jax.random.normal(ks[17], (DEPTH, D_MODEL), f32)
    return {"x": x, "positions": positions, "w_in": w_in, "w_gk2": w_gk2, "b_gk": b_gk, "g_gla": g_gla,
            "p_a": p_a, "p_b": p_b, "w_o": w_o, "ln1_g": ln1_g, "ln1_b": ln1_b,
            "router_w": router_w, "router_b": router_b, "w_gate": w_gate, "w_up": w_up,
            "w_down": w_down, "ln2_g": ln2_g, "ln2_b": ln2_b}


def reference(x, positions, w_in, w_gk2, b_gk, g_gla, p_a, p_b, w_o, ln1_g, ln1_b,
              router_w, router_b, w_gate, w_up, w_down, ln2_g, ln2_b):
    cos, sin = rope_tables(positions)
    h = x
    for l in range(DEPTH):
        mix = hybrid_mixer(h, cos, sin, w_in[l], w_gk2[l], b_gk[l], g_gla[l], p_a[l], p_b[l], w_o[l])
        h = layer_norm(DN_ALPHA * h + mix, ln1_g[l], ln1_b[l])
        ffn = group_moe(h, router_w, router_b, w_gate[l], w_up[l], w_down[l])
        h = layer_norm(DN_ALPHA * h + ffn, ln2_g[l], ln2_b[l])
    return h
```

```python
import functools

import jax
import jax.numpy as jnp
import numpy as np
from jax import lax
from jax.experimental import pallas as pl
from jax.experimental.pallas import tpu as pltpu

HEAD_DIM = 128
DIL_PATTERNS = ((128, 1), (512, 4), (2048, 16))
A_HEADS_PER_GROUP = 4
A_GROUPS = len(DIL_PATTERNS)
A_WIDTH = A_HEADS_PER_GROUP * A_GROUPS * HEAD_DIM
A_OUT = A_HEADS_PER_GROUP * HEAD_DIM
ATT_BLOCK = 128
ROT_DIM = HEAD_DIM // 4
ROPE_THETA = 500000.0
GLA_HEADS = 4
GLA_LOWRANK = 16
GLA_TAU = 16.0
GLA_CHUNK = 64
N_EXPERTS = 16
N_GROUPS = 4
EXPERTS_PER_GROUP = N_EXPERTS // N_GROUPS
TOP_K = 2
LN_EPS = 1e-5
RMS_EPS = 1e-6

LANES = 128
D_MODEL = 2048
MODEL_CHUNKS = D_MODEL // LANES
VMEM_LIMIT = 56 * 1024 * 1024

PROJ_TM = 512
PROJ_TN = 512
ATT_QB = 512
GLA_CB = 256
MERGE_TM = 512
MERGE_TN = 256
OUT_TM = 256
MOE_SUB = 256
MOE_RB = 8
MOE_R = MOE_SUB * MOE_RB
MOE_TF = 128
MOE_TN = 512
SCAT_TM = 256

NEG_BIG = -1e30


def _split_bf16(v):
    hi = v.astype(jnp.bfloat16)
    lo = (v - hi.astype(jnp.float32)).astype(jnp.bfloat16)
    return hi, lo


def _dot(a, b, dims=(((1,), (0,)), ((), ()))):
    return lax.dot_general(a, b, dims, preferred_element_type=jnp.float32)


_NT = (((1,), (1,)), ((), ()))
_TN = (((0,), (0,)), ((), ()))


def _dot3(a, b, dims=(((1,), (0,)), ((), ()))):
    ah, al = _split_bf16(a)
    bh, bl = _split_bf16(b)
    return _dot(ah, bh, dims) + _dot(al, bh, dims) + _dot(ah, bl, dims)


def _layer_norm_rows(z, g, b):
    mu = jnp.mean(z, axis=-1, keepdims=True)
    zc = z - mu
    var = jnp.mean(zc * zc, axis=-1, keepdims=True)
    return zc * lax.rsqrt(var + LN_EPS) * g + b


def _store_chunked(ref, first_chunk, val, pitch=None):
    rows = val.shape[0]
    pitch = pitch or ref.shape[0] // rows
    for c in range(val.shape[1] // LANES):
        ref[pl.ds(first_chunk + c, rows, stride=pitch), :] = val[:, c * LANES:(c + 1) * LANES]


def _load_chunked(ref, rows):
    pitch = ref.shape[0] // rows
    return jnp.concatenate([ref[pl.ds(c, rows, stride=pitch), :] for c in range(pitch)], axis=-1)


def _rope_table_kernel(pos_ref, inv_ref, c_ref, sm_ref, sp_ref):
    ang = pos_ref[...] * inv_ref[...]
    c = jnp.cos(ang)
    s = jnp.sin(ang)
    lane = lax.broadcasted_iota(jnp.int32, ang.shape, 1)
    half = ROT_DIM // 2
    c_ref[...] = jnp.where(lane < ROT_DIM, c, 1.0)
    sm_ref[...] = jnp.where(lane < half, -s, 0.0)
    sp_ref[...] = jnp.where((lane >= half) & (lane < ROT_DIM), s, 0.0)


def _rope_tables(positions):
    T = positions.size
    tm = 1024
    pos = positions.reshape(T, 1).astype(jnp.float32)
    inv = ROPE_THETA ** (-jnp.arange(0, ROT_DIM, 2, dtype=jnp.float32) / ROT_DIM)
    inv_l = jnp.tile(inv, LANES // inv.shape[0]).reshape(1, LANES)
    tab = jax.ShapeDtypeStruct((T, LANES), jnp.float32)
    return pl.pallas_call(
        _rope_table_kernel,
        grid=(T // tm,),
        in_specs=[pl.BlockSpec((tm, 1), lambda i: (i, 0)),
                  pl.BlockSpec((1, LANES), lambda i: (0, 0))],
        out_specs=[pl.BlockSpec((tm, LANES), lambda i: (i, 0))] * 3,
        out_shape=[tab, tab, tab],
        name="rope_tables",
    )(pos, inv_l)


def _in_proj_kernel(n_rope_blocks, h_ref, w_ref, c_ref, sm_ref, sp_ref, wlow_ref, wgk2_ref, bgk_ref,
                    proj_ref, loga_ref, hb_ref, xb_ref):
    j = pl.program_id(1)

    @pl.when(j == 0)
    def _():
        x = h_ref[...]
        xb = x.astype(jnp.bfloat16)
        xb_ref[...] = xb
        hb_ref[...] = xb
        gk_low = _dot3(x, wlow_ref[...])
        gk = _dot3(gk_low, wgk2_ref[...]) + bgk_ref[...]
        log_sig = jnp.minimum(gk, 0.0) - jnp.log(1.0 + jnp.exp(-jnp.abs(gk)))
        loga_ref[...] = log_sig / GLA_TAU

    acc = _dot(xb_ref[...], w_ref[...].astype(jnp.bfloat16))

    @pl.when(j < n_rope_blocks)
    def _():
        c = c_ref[...]
        sm = sm_ref[...]
        sp = sp_ref[...]
        for hh in range(PROJ_TN // HEAD_DIM):
            sl = slice(hh * HEAD_DIM, (hh + 1) * HEAD_DIM)
            xh = acc[:, sl]
            rot = (xh * c + pltpu.roll(xh, HEAD_DIM - ROT_DIM // 2, 1) * sm
                   + pltpu.roll(xh, ROT_DIM // 2, 1) * sp)
            proj_ref[:, sl] = rot.astype(proj_ref.dtype)

    @pl.when(j >= n_rope_blocks)
    def _():
        proj_ref[...] = acc.astype(proj_ref.dtype)


def _in_proj(h, w_in, layer, n_main, tabs, w_low, w_gk2p, b_gk):
    T, D = h.shape
    tm, tn = PROJ_TM, PROJ_TN
    n_rope_blocks = (2 * A_WIDTH) // tn
    gk_dim = w_gk2p.shape[-1]
    kern = functools.partial(_in_proj_kernel, n_rope_blocks)
    return pl.pallas_call(
        kern,
        grid=(T // tm, n_main // tn),
        in_specs=[
            pl.BlockSpec((tm, D), lambda i, j: (i, 0)),
            pl.BlockSpec((None, D, tn), lambda i, j: (layer, 0, j)),
            pl.BlockSpec((tm, LANES), lambda i, j: (i, 0)),
            pl.BlockSpec((tm, LANES), lambda i, j: (i, 0)),
            pl.BlockSpec((tm, LANES), lambda i, j: (i, 0)),
            pl.BlockSpec((D, LANES), lambda i, j: (0, 0)),
            pl.BlockSpec((LANES, gk_dim), lambda i, j: (0, 0)),
            pl.BlockSpec((1, gk_dim), lambda i, j: (0, 0)),
        ],
        out_specs=[
            pl.BlockSpec((tm, tn), lambda i, j: (i, j)),
            pl.BlockSpec((tm, gk_dim), lambda i, j: (i, 0)),
            pl.BlockSpec((tm, D), lambda i, j: (i, 0)),
        ],
        out_shape=[
            jax.ShapeDtypeStruct((T, n_main), jnp.bfloat16),
            jax.ShapeDtypeStruct((T, gk_dim), jnp.float32),
            jax.ShapeDtypeStruct((T, D), jnp.bfloat16),
        ],
        scratch_shapes=[pltpu.VMEM((tm, D), jnp.bfloat16)],
        compiler_params=pltpu.CompilerParams(
            dimension_semantics=("arbitrary", "arbitrary"), vmem_limit_bytes=VMEM_LIMIT),
        name="in_proj",
    )(h, w_in, *tabs, w_low, w_gk2p, b_gk)


def _attn_kernel(qb_rows, q_ref, kc_ref, kp_ref, vc_ref, vp_ref, o_ref, lse_ref, kall_ref, vall_ref):
    i = pl.program_id(2)
    blk = ATT_BLOCK
    kall_ref[0:blk, :] = kp_ref[...]
    kall_ref[blk:, :] = kc_ref[...]
    vall_ref[0:blk, :] = vp_ref[...]
    vall_ref[blk:, :] = vc_ref[...]
    iq = lax.broadcasted_iota(jnp.int32, (blk, 2 * blk), 0)
    jk = lax.broadcasted_iota(jnp.int32, (blk, 2 * blk), 1)
    band = (jk >= iq) & (jk <= iq + blk)
    lane = lax.broadcasted_iota(jnp.int32, (blk, LANES), 1)
    scale = HEAD_DIM ** -0.5
    for sub in range(qb_rows // blk):
        mask = band
        if sub == 0:
            mask = band & ((jk >= blk) | (i > 0))
        lse_tile = jnp.zeros((blk, LANES), jnp.float32)
        for hh in range(A_HEADS_PER_GROUP):
            cs = slice(hh * HEAD_DIM, (hh + 1) * HEAD_DIM)
            q = q_ref[sub * blk:(sub + 1) * blk, cs]
            k = kall_ref[sub * blk:(sub + 2) * blk, cs]
            v = vall_ref[sub * blk:(sub + 2) * blk, cs]
            s = _dot(q, k, _NT) * scale
            s = jnp.where(mask, s, NEG_BIG)
            m = jnp.max(s, axis=-1, keepdims=True)
            p = jnp.exp(s - m)
            den = jnp.sum(p, axis=-1, keepdims=True)
            o = _dot(p.astype(jnp.bfloat16), v) / den
            o_ref[sub * blk:(sub + 1) * blk, cs] = o.astype(o_ref.dtype)
            lse_tile = jnp.where(lane == hh, m + jnp.log(den), lse_tile)
        lse_ref[sub * blk:(sub + 1) * blk, :] = lse_tile


def _dilated_attention(proj, B, S, g, dil):
    T, n_main = proj.shape
    L = S // dil
    qb_rows = min(ATT_QB, L)
    w = A_OUT
    ncol = n_main // w
    view = proj.reshape(B, L, dil * n_main)
    sub_per_step = qb_rows // ATT_BLOCK
    q_col, k_col, v_col = g, A_GROUPS + g, 2 * A_GROUPS + g

    def cur(col):
        return pl.BlockSpec((None, qb_rows, w), lambda b, r, i: (b, i, r * ncol + col))

    def prev(col):
        return pl.BlockSpec((None, ATT_BLOCK, w),
                            lambda b, r, i: (b, jnp.maximum(i * sub_per_step - 1, 0), r * ncol + col))

    kern = functools.partial(_attn_kernel, qb_rows)
    o, lse = pl.pallas_call(
        kern,
        grid=(B, dil, L // qb_rows),
        in_specs=[cur(q_col), cur(k_col), prev(k_col), cur(v_col), prev(v_col)],
        out_specs=[pl.BlockSpec((None, qb_rows, w), lambda b, r, i: (b, i, r)),
                   pl.BlockSpec((None, qb_rows, LANES), lambda b, r, i: (b, i, r))],
        out_shape=[jax.ShapeDtypeStruct((B, L, dil * w), jnp.bfloat16),
                   jax.ShapeDtypeStruct((B, L, dil * LANES), jnp.float32)],
        scratch_shapes=[pltpu.VMEM((qb_rows + ATT_BLOCK, w), jnp.bfloat16),
                        pltpu.VMEM((qb_rows + ATT_BLOCK, w), jnp.bfloat16)],
        compiler_params=pltpu.CompilerParams(
            dimension_semantics=("arbitrary", "arbitrary", "arbitrary"), vmem_limit_bytes=VMEM_LIMIT),
        name=f"dilated_attn_g{g}",
    )(view, view, view, view, view)
    return o.reshape(T, w), lse.reshape(T, LANES)


def _gla_kernel(q_ref, k_ref, v_ref, la_ref, r_ref, g_ref, y_ref, state_ref):
    @pl.when(pl.program_id(2) == 0)
    def _():
        state_ref[...] = jnp.zeros_like(state_ref)

    C = GLA_CHUNK
    dk = q_ref.shape[-1]
    row = lax.broadcasted_iota(jnp.int32, (C, C), 0)
    col = lax.broadcasted_iota(jnp.int32, (C, C), 1)
    causal = col <= row
    tril = jnp.where(causal, 1.0, 0.0).astype(jnp.bfloat16)
    qscale = dk ** -0.5
    for c in range(q_ref.shape[0] // C):
        rs = slice(c * C, (c + 1) * C)
        la = la_ref[rs, :]
        la_hi, la_lo = _split_bf16(la)
        la_lo2 = (la - la_hi.astype(jnp.float32) - la_lo.astype(jnp.float32)).astype(jnp.bfloat16)
        b = _dot(tril, la_hi) + _dot(tril, la_lo) + _dot(tril, la_lo2)
        b_last = b[C - 1:C, :]
        q = q_ref[rs, :].astype(jnp.float32) * qscale
        k = k_ref[rs, :].astype(jnp.float32)
        v = v_ref[rs, :]
        q_t = (q * jnp.exp(b)).astype(jnp.bfloat16)
        k_t = (k * jnp.exp(-b)).astype(jnp.bfloat16)
        k_dec = (k * jnp.exp(b_last - b)).astype(jnp.bfloat16)
        att = jnp.where(causal, _dot(q_t, k_t, _NT), 0.0)
        o = _dot(att.astype(jnp.bfloat16), v)
        st = state_ref[...]
        o = o + _dot(q_t, st.astype(jnp.bfloat16), _NT)
        state_ref[...] = st * jnp.exp(b_last) + _dot(v, k_dec, _TN)
        o = o * lax.rsqrt(jnp.mean(o * o, axis=-1, keepdims=True) + RMS_EPS) * g_ref[...]
        r = r_ref[rs, :].astype(jnp.float32)
        y_ref[rs, :] = (o * (r * jax.nn.sigmoid(r))).astype(y_ref.dtype)


def _gla(proj, log_a, g_gla, B, S, col_q, col_k, col_v, col_r):
    T, _ = proj.shape
    gk_dim = log_a.shape[-1]
    dk = gk_dim // GLA_HEADS
    dv = g_gla.shape[-1]
    cb = GLA_CB
    nblk = S // cb
    qo, ko, vo, ro = col_q // dk, col_k // dk, col_v // dv, col_r // dv

    def rows(width, off):
        return pl.BlockSpec((cb, width), lambda b, hh, i: (b * nblk + i, off + hh))

    return pl.pallas_call(
        _gla_kernel,
        grid=(B, GLA_HEADS, nblk),
        in_specs=[rows(dk, qo), rows(dk, ko), rows(dv, vo), rows(dk, 0), rows(dv, ro),
                  pl.BlockSpec((1, dv), lambda b, hh, i: (0, 0))],
        out_specs=rows(dv, 0),
        out_shape=jax.ShapeDtypeStruct((T, GLA_HEADS * dv), jnp.bfloat16),
        scratch_shapes=[pltpu.VMEM((dv, dk), jnp.float32)],
        compiler_params=pltpu.CompilerParams(
            dimension_semantics=("arbitrary", "arbitrary", "arbitrary"), vmem_limit_bytes=VMEM_LIMIT),
        name="gla",
    )(proj, proj, proj, log_a, proj, g_gla)


def _merge_kernel(hb_ref, o1_ref, o2_ref, o3_ref, l1_ref, l2_ref, l3_ref, yb_ref,
                  wga_ref, wgb_ref, pa_ref, pb_ref, out_ref, ya_ref):
    @pl.when(pl.program_id(1) == 0)
    def _():
        l1, l2, l3 = l1_ref[...], l2_ref[...], l3_ref[...]
        m = jnp.maximum(jnp.maximum(l1, l2), l3)
        e1, e2, e3 = jnp.exp(l1 - m), jnp.exp(l2 - m), jnp.exp(l3 - m)
        tot = e1 + e2 + e3
        for hh in range(A_HEADS_PER_GROUP):
            cs = slice(hh * HEAD_DIM, (hh + 1) * HEAD_DIM)
            w1 = (e1 / tot)[:, hh:hh + 1]
            w2 = (e2 / tot)[:, hh:hh + 1]
            w3 = (e3 / tot)[:, hh:hh + 1]
            ya = (w1 * o1_ref[:, cs].astype(jnp.float32) + w2 * o2_ref[:, cs].astype(jnp.float32)
                  + w3 * o3_ref[:, cs].astype(jnp.float32))
            ya_ref[:, cs] = ya.astype(ya_ref.dtype)

    hb = hb_ref[...]
    ga = _dot(hb, wga_ref[...])
    gb = _dot(hb, wgb_ref[...])
    za = _dot(ya_ref[...], pa_ref[...])
    zb = _dot(yb_ref[...], pb_ref[...])
    out_ref[...] = (jax.nn.sigmoid(ga) * za + jax.nn.sigmoid(gb) * zb).astype(out_ref.dtype)


def _merge(hb, os_, lses, yb, w_ga, w_gb, p_a, p_b):
    T, D = hb.shape
    tm, tn = MERGE_TM, MERGE_TN
    row = lambda width: pl.BlockSpec((tm, width), lambda i, j: (i, 0))
    colw = lambda k: pl.BlockSpec((k, tn), lambda i, j: (0, j))
    return pl.pallas_call(
        _merge_kernel,
        grid=(T // tm, D // tn),
        in_specs=[row(D), row(A_OUT), row(A_OUT), row(A_OUT), row(LANES), row(LANES), row(LANES),
                  row(yb.shape[1]), colw(D), colw(D), colw(A_OUT), colw(yb.shape[1])],
        out_specs=pl.BlockSpec((tm, tn), lambda i, j: (i, j)),
        out_shape=jax.ShapeDtypeStruct((T, D), jnp.bfloat16),
        scratch_shapes=[pltpu.VMEM((tm, A_OUT), jnp.bfloat16)],
        compiler_params=pltpu.CompilerParams(
            dimension_semantics=("arbitrary", "arbitrary"), vmem_limit_bytes=VMEM_LIMIT),
        name="merge",
    )(hb, *os_, *lses, yb, w_ga, w_gb, p_a, p_b)


def _route(h1, rwt, rb, carry_ref, eid_ref, wt_ref, rank_ref, cnt_ref):
    tm = h1.shape[0]
    logits = _dot3(rwt, h1, _NT) + rb
    mx = jnp.max(logits, axis=0, keepdims=True)
    ex = jnp.exp(logits - mx)
    probs = ex / jnp.sum(ex, axis=0, keepdims=True)
    eidx = lax.broadcasted_iota(jnp.int32, probs.shape, 0).astype(jnp.float32)
    big = float(N_EXPERTS)

    def top2(mask):
        p0 = jnp.where(mask, probs, -1.0)
        v1 = jnp.max(p0, axis=0, keepdims=True)
        i1 = jnp.min(jnp.where(p0 == v1, eidx, big), axis=0, keepdims=True)
        p1 = jnp.where(eidx == i1, -1.0, p0)
        v2 = jnp.max(p1, axis=0, keepdims=True)
        i2 = jnp.min(jnp.where(p1 == v2, eidx, big), axis=0, keepdims=True)
        return v1, i1, v2, i2

    best = None
    for grp in range(N_GROUPS):
        lo_e, hi_e = grp * EXPERTS_PER_GROUP, (grp + 1) * EXPERTS_PER_GROUP
        v1, i1, v2, i2 = top2((eidx >= lo_e) & (eidx < hi_e))
        score = v1 + v2
        if best is None:
            best = (score, v1, i1, v2, i2)
        else:
            take = score > best[0]
            best = tuple(jnp.where(take, n, o) for n, o in zip((score, v1, i1, v2, i2), best))
    _, v1, i1, v2, i2 = best
    tot = v1 + v2
    eid_ref[0:1, :] = i1.astype(jnp.int32)
    eid_ref[1:2, :] = i2.astype(jnp.int32)
    wt_ref[0:1, :] = v1 / tot
    wt_ref[1:2, :] = v2 / tot

    hot1 = eidx == i1
    hot2 = eidx == i2
    multi = jnp.where(hot1 | hot2, 1.0, 0.0).astype(jnp.bfloat16)
    r_i = lax.broadcasted_iota(jnp.int32, (tm, tm), 0)
    c_i = lax.broadcasted_iota(jnp.int32, (tm, tm), 1)
    before = jnp.where(r_i < c_i, 1.0, 0.0).astype(jnp.bfloat16)
    prefix = _dot(multi, before) + carry_ref[...]
    rank_ref[0:1, :] = jnp.sum(jnp.where(hot1, prefix, 0.0), axis=0, keepdims=True).astype(jnp.int32)
    rank_ref[1:2, :] = jnp.sum(jnp.where(hot2, prefix, 0.0), axis=0, keepdims=True).astype(jnp.int32)
    new_carry = carry_ref[...] + jnp.sum(multi.astype(jnp.float32), axis=1, keepdims=True)
    carry_ref[...] = new_carry
    cnt_ref[...] = jnp.broadcast_to(new_carry, cnt_ref.shape).astype(jnp.int32)


def _out_proj_kernel(alpha, m_ref, wo_ref, h_ref, g_ref, b_ref, rwt_ref, rb_ref,
                     h1_ref, h1c_ref, eid_ref, wt_ref, rank_ref, cnt_ref, carry_ref):
    @pl.when(pl.program_id(0) == 0)
    def _():
        carry_ref[...] = jnp.zeros_like(carry_ref)

    mix = _dot(m_ref[...], wo_ref[...])
    h1 = _layer_norm_rows(alpha * h_ref[...] + mix, g_ref[...], b_ref[...])
    h1_ref[...] = h1
    _store_chunked(h1c_ref, 0, h1)
    _route(h1, rwt_ref[...], rb_ref[...], carry_ref, eid_ref, wt_ref, rank_ref, cnt_ref)


def _out_proj_route(merged, w_o, h, ln_g, ln_b, rwt, rb, alpha):
    T, D = h.shape
    tm = OUT_TM
    E = rwt.shape[0]
    full = lambda shp: pl.BlockSpec(shp, lambda i: (0, 0))
    tok = lambda rows: pl.BlockSpec((rows, tm), lambda i: (0, i))
    kern = functools.partial(_out_proj_kernel, alpha)
    return pl.pallas_call(
        kern,
        grid=(T // tm,),
        in_specs=[pl.BlockSpec((tm, D), lambda i: (i, 0)), full((D, D)),
                  pl.BlockSpec((tm, D), lambda i: (i, 0)), full((1, D)), full((1, D)),
                  full((E, D)), full((E, 1))],
        out_specs=[pl.BlockSpec((tm, D), lambda i: (i, 0)),
                   pl.BlockSpec((tm * (D // LANES), LANES), lambda i: (i, 0)),
                   tok(TOP_K), tok(TOP_K), tok(TOP_K), full((E, LANES))],
        out_shape=[jax.ShapeDtypeStruct((T, D), jnp.float32),
                   jax.ShapeDtypeStruct((T * (D // LANES), LANES), jnp.float32),
                   jax.ShapeDtypeStruct((TOP_K, T), jnp.int32),
                   jax.ShapeDtypeStruct((TOP_K, T), jnp.float32),
                   jax.ShapeDtypeStruct((TOP_K, T), jnp.int32),
                   jax.ShapeDtypeStruct((E, LANES), jnp.int32)],
        scratch_shapes=[pltpu.VMEM((E, 1), jnp.float32)],
        compiler_params=pltpu.CompilerParams(
            dimension_semantics=("arbitrary",), vmem_limit_bytes=VMEM_LIMIT),
        name="out_proj_route",
    )(merged, w_o, h, ln_g, ln_b, rwt, rb)


def _routing_tables(eid, rank, counts, n_blocks):
    nsub = (counts + MOE_SUB - 1) // MOE_SUB
    padded = nsub * MOE_SUB
    seg_start = jnp.cumsum(padded) - padded
    pos = jnp.take(seg_start, eid) + rank
    nblk_e = (nsub + MOE_RB - 1) // MOE_RB
    blk_cum = jnp.cumsum(nblk_e)
    total = blk_cum[-1]
    b = jnp.arange(n_blocks, dtype=jnp.int32)
    b_eff = jnp.minimum(b, total - 1)
    e_b = jnp.sum((blk_cum[None, :] <= b_eff[:, None]).astype(jnp.int32), axis=1)
    local = b_eff - (jnp.take(blk_cum, e_b) - jnp.take(nblk_e, e_b))
    active = b < total
    row0 = jnp.where(active, jnp.take(seg_start, e_b) + local * MOE_R, 0)
    nsub_b = jnp.where(active, jnp.clip(jnp.take(nsub, e_b) - local * MOE_RB, 0, MOE_RB), 0)
    nvalid_b = jnp.where(active, jnp.clip(jnp.take(counts, e_b) - local * MOE_R, 0, MOE_R), 0)
    i32 = lambda a: a.astype(jnp.int32)
    return i32(pos), i32(e_b), i32(row0), i32(nsub_b), i32(nvalid_b)


def _row_slice(ref, r, pitch):
    return ref.at[pl.ds(pl.multiple_of(r * pitch, pitch), pitch), :]


def _row_copy(src_ref, dst_ref, sem, s, d):
    pitch = MODEL_CHUNKS
    return pltpu.make_async_copy(_row_slice(src_ref, s, pitch), _row_slice(dst_ref, d, pitch), sem)


def _scatter_kernel(pos0_ref, pos1_ref, h_ref, xs_ref, sem):
    base = pl.program_id(0) * SCAT_TM

    def issue(t, carry):
        _row_copy(h_ref, xs_ref, sem, base + t, pos0_ref[base + t]).start()
        _row_copy(h_ref, xs_ref, sem, base + t, pos1_ref[base + t]).start()
        return carry

    lax.fori_loop(0, SCAT_TM, issue, 0)

    def drain(t, carry):
        _row_copy(h_ref, xs_ref, sem, 0, 0).wait()
        _row_copy(h_ref, xs_ref, sem, 0, 0).wait()
        return carry

    lax.fori_loop(0, SCAT_TM, drain, 0)


def _scatter_rows(h1c, pos, p_alloc):
    T = pos.shape[1]
    return pl.pallas_call(
        _scatter_kernel,
        grid_spec=pltpu.PrefetchScalarGridSpec(
            num_scalar_prefetch=2,
            grid=(T // SCAT_TM,),
            in_specs=[pl.BlockSpec(memory_space=pl.ANY)],
            out_specs=pl.BlockSpec(memory_space=pl.ANY),
            scratch_shapes=[pltpu.SemaphoreType.DMA(())],
        ),
        out_shape=jax.ShapeDtypeStruct((p_alloc * MODEL_CHUNKS, LANES), jnp.float32),
        compiler_params=pltpu.CompilerParams(dimension_semantics=("arbitrary",)),
        name="scatter_rows",
    )(pos[0], pos[1], h1c)


def _moe_kernel(n_f, be_ref, row0_ref, nsub_ref, nvalid_ref,
                xs_ref, wg_ref, wu_ref, wd_ref, out_ref,
                xbuf, xstage, hbuf, obuf, wcat, wdb, in_sem, out_sem):
    b = pl.program_id(0)
    s = pl.program_id(1)
    nsub = nsub_ref[b]
    row0 = row0_ref[b]
    nvalid = nvalid_ref[b]
    sub = MOE_SUB
    pitch = MODEL_CHUNKS
    tf = wg_ref.shape[-1]
    d_model = wd_ref.shape[-1]

    def sorted_rows(ref, j):
        start = pl.multiple_of((row0 + j * sub) * pitch, sub * pitch)
        return ref.at[pl.ds(start, sub * pitch), :]

    def x_copy(j, slot):
        return pltpu.make_async_copy(sorted_rows(xs_ref, j), xstage.at[slot], in_sem.at[slot])

    def o_copy(j, slot):
        return pltpu.make_async_copy(obuf.at[slot], sorted_rows(out_ref, j), out_sem.at[slot])

    @pl.when((s == 0) & (nsub > 0))
    def _():
        x_copy(0, 0).start()
        for j in range(MOE_RB):
            @pl.when(j < nsub)
            def _():
                if j + 1 < MOE_RB:
                    @pl.when(j + 1 < nsub)
                    def _():
                        x_copy(j + 1, (j + 1) % 2).start()
                x_copy(j, j % 2).wait()
                rid = lax.broadcasted_iota(jnp.int32, (sub, 1), 0) + j * sub
                x = jnp.where(rid < nvalid, _load_chunked(xstage.at[j % 2], sub), 0.0)
                xbuf[j * sub:(j + 1) * sub, :] = x.astype(jnp.bfloat16)

    @pl.when(nsub > 0)
    def _():
        wcat[:, 0:tf] = wg_ref[...].astype(jnp.bfloat16)
        wcat[:, tf:] = wu_ref[...].astype(jnp.bfloat16)
        wdb[pl.ds(pl.multiple_of(s * tf, tf), tf), :] = wd_ref[...].astype(jnp.bfloat16)

        def body(j, carry):
            r0 = pl.multiple_of(j * sub, sub)
            au = _dot(xbuf[pl.ds(r0, sub), :], wcat[...])
            a = au[:, 0:tf]
            u = au[:, tf:]
            hbuf[s, pl.ds(r0, sub), :] = (a * jax.nn.sigmoid(a) * u).astype(jnp.bfloat16)
            return carry

        lax.fori_loop(0, nsub, body, 0)

    @pl.when((s == n_f - 1) & (nsub > 0))
    def _():
        def body(j, carry):
            slot = j % 2

            @pl.when(j >= 2)
            def _():
                o_copy(j - 2, slot).wait()

            r0 = pl.multiple_of(j * sub, sub)
            hcat = jnp.concatenate([hbuf[f, pl.ds(r0, sub), :] for f in range(n_f)], axis=-1)
            for c in range(d_model // MOE_TN):
                o = _dot(hcat, wdb[:, c * MOE_TN:(c + 1) * MOE_TN])
                _store_chunked(obuf.at[slot], c * (MOE_TN // LANES), o, pitch)
            o_copy(j, slot).start()
            return carry

        lax.fori_loop(0, nsub, body, 0)

        @pl.when(nsub >= 2)
        def _():
            o_copy(nsub - 2, nsub % 2).wait()

        o_copy(nsub - 1, (nsub - 1) % 2).wait()


def _moe(xs, w_gate, w_up, w_down, layer, be, row0, nsub, nvalid):
    D = w_gate.shape[-2]
    F = w_gate.shape[-1]
    n_f = F // MOE_TF
    n_blocks = be.shape[0]
    stage_rows = MOE_SUB * MODEL_CHUNKS
    kern = functools.partial(_moe_kernel, n_f)

    def f_of(b, s, nsub_r):
        return jnp.where(nsub_r[b] > 0, s, n_f - 1)

    def gu_map(b, s, be_r, row0_r, nsub_r, nvalid_r):
        return (layer, be_r[b], 0, f_of(b, s, nsub_r))

    def d_map(b, s, be_r, row0_r, nsub_r, nvalid_r):
        return (layer, be_r[b], f_of(b, s, nsub_r), 0)

    return pl.pallas_call(
        kern,
        grid_spec=pltpu.PrefetchScalarGridSpec(
            num_scalar_prefetch=4,
            grid=(n_blocks, n_f),
            in_specs=[pl.BlockSpec(memory_space=pl.ANY),
                      pl.BlockSpec((None, None, D, MOE_TF), gu_map),
                      pl.BlockSpec((None, None, D, MOE_TF), gu_map),
                      pl.BlockSpec((None, None, MOE_TF, D), d_map)],
            out_specs=pl.BlockSpec(memory_space=pl.ANY),
            scratch_shapes=[pltpu.VMEM((MOE_R, D), jnp.bfloat16),
                            pltpu.VMEM((2, stage_rows, LANES), jnp.float32),
                            pltpu.VMEM((n_f, MOE_R, MOE_TF), jnp.bfloat16),
                            pltpu.VMEM((2, stage_rows, LANES), jnp.float32),
                            pltpu.VMEM((D, 2 * MOE_TF), jnp.bfloat16),
                            pltpu.VMEM((F, D), jnp.bfloat16),
                            pltpu.SemaphoreType.DMA((2,)),
                            pltpu.SemaphoreType.DMA((2,))],
        ),
        out_shape=jax.ShapeDtypeStruct(xs.shape, jnp.float32),
        compiler_params=pltpu.CompilerParams(
            dimension_semantics=("arbitrary", "arbitrary"), vmem_limit_bytes=VMEM_LIMIT),
        name="moe_ffn",
    )(be, row0, nsub, nvalid, xs, w_gate, w_up, w_down)


def _combine_kernel(alpha, pos0_ref, pos1_ref, ys_ref, h_ref, wt_ref, g_ref, b_ref, out_ref, gbuf, sem):
    base = pl.program_id(0) * SCAT_TM

    def issue(t, carry):
        _row_copy(ys_ref, gbuf.at[0], sem, pos0_ref[base + t], t).start()
        _row_copy(ys_ref, gbuf.at[1], sem, pos1_ref[base + t], t).start()
        return carry

    lax.fori_loop(0, SCAT_TM, issue, 0)

    def drain(t, carry):
        _row_copy(ys_ref, gbuf.at[0], sem, 0, 0).wait()
        _row_copy(ys_ref, gbuf.at[1], sem, 0, 0).wait()
        return carry

    lax.fori_loop(0, SCAT_TM, drain, 0)
    wt = wt_ref[...]
    y = wt[:, 0:1] * _load_chunked(gbuf.at[0], SCAT_TM) + wt[:, 1:2] * _load_chunked(gbuf.at[1], SCAT_TM)
    out_ref[...] = _layer_norm_rows(alpha * h_ref[...] + y, g_ref[...], b_ref[...])


def _combine(ys, pos, h1, wt_rows, ln_g, ln_b, alpha):
    T, D = h1.shape
    tm = SCAT_TM
    kern = functools.partial(_combine_kernel, alpha)
    return pl.pallas_call(
        kern,
        grid_spec=pltpu.PrefetchScalarGridSpec(
            num_scalar_prefetch=2,
            grid=(T // tm,),
            in_specs=[pl.BlockSpec(memory_space=pl.ANY),
                      pl.BlockSpec((tm, D), lambda i, p0, p1: (i, 0)),
                      pl.BlockSpec((tm, TOP_K), lambda i, p0, p1: (i, 0)),
                      pl.BlockSpec((1, D), lambda i, p0, p1: (0, 0)),
                      pl.BlockSpec((1, D), lambda i, p0, p1: (0, 0))],
            out_specs=pl.BlockSpec((tm, D), lambda i, p0, p1: (i, 0)),
            scratch_shapes=[pltpu.VMEM((TOP_K, tm * MODEL_CHUNKS, LANES), jnp.float32),
                            pltpu.SemaphoreType.DMA(())],
        ),
        out_shape=jax.ShapeDtypeStruct((T, D), jnp.float32),
        compiler_params=pltpu.CompilerParams(
            dimension_semantics=("arbitrary",), vmem_limit_bytes=VMEM_LIMIT),
        name="combine_ln",
    )(pos[0], pos[1], ys, h1, wt_rows, ln_g, ln_b)


def kernel(x, positions, w_in, w_gk2, b_gk, g_gla, p_a, p_b, w_o, ln1_g, ln1_b,
           router_w, router_b, w_gate, w_up, w_down, ln2_g, ln2_b):
    B, S, D = x.shape
    T = B * S
    depth = w_in.shape[0]
    gla_dk = w_gk2.shape[-1]
    gla_dv = g_gla.shape[-1] * GLA_HEADS
    alpha = (2 * depth) ** 0.25
    bf = jnp.bfloat16

    sizes = (A_WIDTH, A_WIDTH, A_WIDTH, gla_dk, gla_dk, gla_dv, gla_dv, GLA_LOWRANK, D, D)
    offs = np.concatenate([[0], np.cumsum(sizes)])
    col_qb, col_kb, col_vb, col_rb, col_low, col_ga, col_gb = (int(offs[k]) for k in (3, 4, 5, 6, 7, 8, 9))
    n_main = col_low

    tabs = _rope_tables(positions)
    rwt = router_w.T
    rb = router_b.reshape(-1, 1)
    n_sub_max = (T * TOP_K) // MOE_SUB + N_EXPERTS
    p_alloc = n_sub_max * MOE_SUB
    n_blocks = -(-n_sub_max // MOE_RB) + N_EXPERTS

    h = x.reshape(T, D)
    for l in range(depth):
        w_low = jnp.pad(w_in[l, :, col_low:col_low + GLA_LOWRANK], ((0, 0), (0, LANES - GLA_LOWRANK)))
        w_gk2p = jnp.pad(w_gk2[l], ((0, LANES - GLA_LOWRANK), (0, 0)))
        w_ga = w_in[l, :, col_ga:col_ga + D].astype(bf)
        w_gb = w_in[l, :, col_gb:col_gb + D].astype(bf)

        proj, log_a, hb = _in_proj(h, w_in, l, n_main, tabs, w_low, w_gk2p, b_gk[l].reshape(1, -1))
        outs, lses = [], []
        for g, (_, dil) in enumerate(DIL_PATTERNS):
            o_g, lse_g = _dilated_attention(proj, B, S, g, dil)
            outs.append(o_g)
            lses.append(lse_g)
        y_b = _gla(proj, log_a, g_gla[l].reshape(1, -1), B, S, col_qb, col_kb, col_vb, col_rb)
        merged = _merge(hb, outs, lses, y_b, w_ga, w_gb, p_a[l].astype(bf), p_b[l].astype(bf))
        h1, h1c, eid, wts, rank, cnt = _out_proj_route(
            merged, w_o[l].astype(bf), h, ln1_g[l].reshape(1, -1), ln1_b[l].reshape(1, -1), rwt, rb, alpha)

        pos, be, row0, nsub, nvalid = _routing_tables(eid, rank, cnt[:, 0], n_blocks)
        xs = _scatter_rows(h1c, pos, p_alloc)
        ys = _moe(xs, w_gate, w_up, w_down, l, be, row0, nsub, nvalid)
        h = _combine(ys, pos, h1, wts.T, ln2_g[l].reshape(1, -1), ln2_b[l].reshape(1, -1), alpha)
    return h.reshape(B, S, D)
```
